```python
import math
import jax, jax.numpy as jnp
from jax import lax
import numpy as np

D_MODEL = 1024
BATCH = 8
SEQ = 2048
DEPTH = 4
DEC_BATCH = 32
DEC_SEQ = 4
PAST_LEN = 16384
PAGE_SIZE = 128

MLA_HEADS = 8
MLA_NOPE = 64
MLA_ROPE = 32
MLA_VDIM = 64
MLA_Q_RANK = 256
MLA_KV_RANK = 256
MLA_WIDTH = MLA_HEADS * MLA_VDIM
ROPE_THETA = 10000.0
NSA_HEADS = 8
NSA_GROUPS = 2
NSA_HPG = NSA_HEADS // NSA_GROUPS
NSA_HD = 64
NSA_WIDTH = NSA_HEADS * NSA_HD
CMP_BLOCK = 32
CMP_STRIDE = 16
CMP_HIDDEN = 64
SLC_BLOCK = 64
N_SELECT = 16
WINDOW = 512
REL_BUCKETS = 32
REL_MAX_DIST = 1024
QBLK = 128
NORM_EPS = 1e-6
MASK_NEG = -1e30
FORCE_BONUS = 1000.0

IN_SPLITS = (MLA_Q_RANK, MLA_KV_RANK, MLA_ROPE, MLA_WIDTH,
             NSA_WIDTH, 6 * NSA_GROUPS * NSA_HD, 3 * NSA_HEADS, NSA_WIDTH,
             2 * D_MODEL)
IN_WIDTH = sum(IN_SPLITS)
IN_OFFSETS = tuple(sum(IN_SPLITS[:i + 1]) for i in range(len(IN_SPLITS) - 1))

kernel_name = 'mla_nsa_gated_hybrid_decode_step'


def rmsnorm(x, g):
    xf = x.astype(jnp.float32)
    y = xf * lax.rsqrt(jnp.mean(xf * xf, axis=-1, keepdims=True) + NORM_EPS)
    return (y * g.astype(jnp.float32)).astype(x.dtype)


def rope(x, pos):
    half = MLA_ROPE // 2
    freq = ROPE_THETA ** (-jnp.arange(half, dtype=jnp.float32) / half)
    ang = pos.astype(jnp.float32)[:, None] * freq[None, :]
    ang = ang.reshape((1, ang.shape[0]) + (1,) * (x.ndim - 3) + (half,))
    cos, sin = jnp.cos(ang), jnp.sin(ang)
    x1 = x[..., :half].astype(jnp.float32)
    x2 = x[..., half:].astype(jnp.float32)
    return jnp.concatenate([x1 * cos - x2 * sin, x2 * cos + x1 * sin], axis=-1).astype(x.dtype)


def rel_bucket(dist):
    n = jnp.maximum(dist, 0)
    exact = REL_BUCKETS // 2
    nf = jnp.maximum(n, exact).astype(jnp.float32)
    big = exact + (jnp.log(nf / exact) / math.log(REL_MAX_DIST / exact)
                   * (REL_BUCKETS - exact)).astype(jnp.int32)
    return jnp.where(n < exact, n, jnp.minimum(big, REL_BUCKETS - 1))


def masked_softmax(s, mask):
    s = jnp.where(mask, s, MASK_NEG)
    e = jnp.where(mask, jnp.exp(s - jnp.max(s, axis=-1, keepdims=True)), 0.0)
    return e / jnp.maximum(jnp.sum(e, axis=-1, keepdims=True), 1e-30)


def q_block_map(fn, arrays, q_pos):
    T = q_pos.shape[0]
    blk = min(QBLK, T)
    nb = T // blk
    split = lambda a: jnp.moveaxis(a.reshape((a.shape[0], nb, blk) + a.shape[2:]), 1, 0)
    out = lax.map(lambda xs: fn(*xs), tuple(split(a) for a in arrays) + (q_pos.reshape(nb, blk),))
    out = jnp.moveaxis(out, 0, 1)
    return out.reshape((out.shape[0], T) + out.shape[3:])


def mla_attend(cq, ckv_all, kpe_all, q_pos, q_norm, w_uq, w_uk, w_uv):
    B, T, _ = cq.shape
    q = (rmsnorm(cq, q_norm) @ w_uq).reshape(B, T, MLA_HEADS, MLA_NOPE + MLA_ROPE)
    q_nope = q[..., :MLA_NOPE]
    q_pe = rope(q[..., MLA_NOPE:], q_pos)
    q_lat = jnp.einsum('bthn,rhn->bthr', q_nope, w_uk)
    k_pos = jnp.arange(ckv_all.shape[1])
    scale = (MLA_NOPE + MLA_ROPE) ** -0.5

    def block(ql, qp, pos):
        s = (jnp.einsum('bthr,bsr->bths', ql, ckv_all)
             + jnp.einsum('bthp,bsp->bths', qp, kpe_all)).astype(jnp.float32) * scale
        mask = (k_pos[None, :] <= pos[:, None])[None, :, None, :]
        p = masked_softmax(s, mask)
        return jnp.einsum('bths,bsr->bthr', p.astype(ckv_all.dtype), ckv_all)

    o_lat = q_block_map(block, (q_lat, q_pe), q_pos)
    return jnp.einsum('bthr,rhv->bthv', o_lat, w_uv).reshape(B, T, MLA_WIDTH)


def compress_kv(raw, w1, w2, pe):
    B, S = raw.shape[:2]
    nch = S // CMP_STRIDE
    ch = raw[:, :nch * CMP_STRIDE].reshape(B, nch, CMP_STRIDE, NSA_GROUPS, NSA_HD)
    h = (jnp.einsum('bnlgd,ldf->bngf', ch[:, :-1], w1[:CMP_STRIDE])
         + jnp.einsum('bnlgd,ldf->bngf', ch[:, 1:], w1[CMP_STRIDE:])
         + jnp.einsum('ld,ldf->f', pe, w1))
    return jnp.einsum('bngf,fd->bngd', jax.nn.silu(h), w2)


def nsa_attend(q, gates, q_pos, kv_cmp, kv_slc, kv_win, win_pos0, cmp_w1, cmp_w2, cmp_pe, rel_table):
    B = q.shape[0]
    S = kv_cmp.shape[1]
    scale = NSA_HD ** -0.5
    ck = compress_kv(kv_cmp[:, :, 0], cmp_w1[0], cmp_w2[0], cmp_pe[0])
    cv = compress_kv(kv_cmp[:, :, 1], cmp_w1[1], cmp_w2[1], cmp_pe[1])
    n_cmp = ck.shape[1]
    c_end = jnp.arange(n_cmp) * CMP_STRIDE + CMP_BLOCK - 1
    n_slc = -(-S // SLC_BLOCK)
    k_sel = min(N_SELECT, n_slc)
    cmp_per_slc = SLC_BLOCK // CMP_STRIDE
    slc = jnp.pad(kv_slc, ((0, 0), (0, n_slc * SLC_BLOCK - S), (0, 0), (0, 0), (0, 0)))
    slc = slc.reshape(B, n_slc, SLC_BLOCK, 2, NSA_GROUPS, NSA_HD).transpose(3, 0, 4, 1, 2, 5)
    ks_blocks, vs_blocks = slc[0], slc[1]
    win = jnp.pad(kv_win, ((0, 0), (WINDOW, 0), (0, 0), (0, 0), (0, 0)))
    table = rel_table.reshape(REL_BUCKETS, NSA_GROUPS, NSA_HPG)
    bi = jnp.arange(B)[:, None, None, None]
    gi = jnp.arange(NSA_GROUPS)[None, None, :, None]
    slc_ids = jnp.arange(n_slc)

    def block(qb, gb, pos):
        nq = qb.shape[1]
        qg = qb.reshape(B, nq, NSA_GROUPS, NSA_HPG, NSA_HD)
        dist_c = pos[:, None] - c_end[None, :]
        bias_c = jnp.transpose(table[rel_bucket(dist_c)], (0, 2, 3, 1))
        s_c = jnp.einsum('btghd,bngd->btghn', qg, ck).astype(jnp.float32) * scale + bias_c
        p_c = masked_softmax(s_c, (dist_c >= 0)[:, None, None, :])
        o_c = jnp.einsum('btghn,bngd->btghd', p_c.astype(cv.dtype), cv)
        imp = jnp.sum(p_c, axis=3)
        imp = jnp.pad(imp, ((0, 0), (0, 0), (0, 0), (0, n_slc * cmp_per_slc - n_cmp)))
        imp = imp.reshape(B, nq, NSA_GROUPS, n_slc, cmp_per_slc).sum(-1)
        cur = (pos // SLC_BLOCK)[:, None]
        forced = (slc_ids == 0) | (slc_ids == cur) | (slc_ids == cur - 1)
        valid = slc_ids[None, :] * SLC_BLOCK <= pos[:, None]
        score = jnp.where(valid[None, :, None, :],
                          imp + FORCE_BONUS * forced[None, :, None, :].astype(imp.dtype), -1.0)
        _, idx = lax.top_k(score, k_sel)
        ks = ks_blocks[bi, gi, idx]
        vs = vs_blocks[bi, gi, idx]
        kpos = idx[..., None] * SLC_BLOCK + jnp.arange(SLC_BLOCK)
        dist_s = pos[None, :, None, None, None] - kpos
        bias_s = jnp.moveaxis(table[rel_bucket(dist_s), gi[..., None]], -1, 3)
        s_s = jnp.einsum('btghd,btgkld->btghkl', qg, ks).astype(jnp.float32) * scale + bias_s
        s_s = s_s.reshape(B, nq, NSA_GROUPS, NSA_HPG, k_sel * SLC_BLOCK)
        m_s = (dist_s >= 0).reshape(B, nq, NSA_GROUPS, 1, k_sel * SLC_BLOCK)
        p_s = masked_softmax(s_s, m_s).reshape(B, nq, NSA_GROUPS, NSA_HPG, k_sel, SLC_BLOCK)
        o_s = jnp.einsum('btghkl,btgkld->btghd', p_s.astype(vs.dtype), vs)
        n_w = WINDOW + nq
        wk = lax.dynamic_slice_in_dim(win, pos[0] - win_pos0, n_w, axis=1)
        kpos_w = pos[0] - WINDOW + jnp.arange(n_w)
        dist_w = pos[:, None] - kpos_w[None, :]
        m_w = (dist_w >= 0) & (dist_w < WINDOW) & (kpos_w >= 0)[None, :]
        bias_w = jnp.transpose(table[rel_bucket(dist_w)], (0, 2, 3, 1))
        s_w = jnp.einsum('btghd,bsgd->btghs', qg, wk[:, :, 0]).astype(jnp.float32) * scale + bias_w
        p_w = masked_softmax(s_w, m_w[:, None, None, :])
        o_w = jnp.einsum('btghs,bsgd->btghd', p_w.astype(wk.dtype), wk[:, :, 1])
        g = gb.reshape(B, nq, NSA_GROUPS, NSA_HPG, 3)
        o = g[..., 0:1] * o_c + g[..., 1:2] * o_s + g[..., 2:3] * o_w
        return o.reshape(B, nq, NSA_WIDTH)

    return q_block_map(block, (q, gates), q_pos)


def layer_forward(x, c, pos0, past, lw, rel_table):
    B, T, _ = x.shape
    q_pos = pos0 + jnp.arange(T)
    shift, scale, gate = jnp.split(jax.nn.silu(c) @ lw['w_ada'] + lw['b_ada'], 3, axis=-1)
    h = rmsnorm(x, lw['norm']) * (1.0 + scale[:, None]) + shift[:, None]
    z = h @ lw['w_in']
    cq, ckv, kpe, g_mla, q_nsa, kv_nsa, br_gate, g_nsa, merge = jnp.split(z, IN_OFFSETS, axis=-1)
    ckv = rmsnorm(ckv, lw['kv_norm'])
    kpe = rope(kpe, q_pos)
    kv_nsa = kv_nsa.reshape(B, T, 3, 2, NSA_GROUPS, NSA_HD)
    cmp_new, slc_new, win_new = kv_nsa[:, :, 0], kv_nsa[:, :, 1], kv_nsa[:, :, 2]
    if past is None:
        ckv_all, kpe_all, cmp_all, slc_all, win_all = ckv, kpe, cmp_new, slc_new, win_new
        win_pos0 = pos0
        win_state = win_new[:, -min(WINDOW, T):]
    else:
        ckv_p, kpe_p, cmp_p, slc_p, win_p = past
        ckv_all = jnp.concatenate([ckv_p, ckv], axis=1)
        kpe_all = jnp.concatenate([kpe_p, kpe], axis=1)
        cmp_all = jnp.concatenate([cmp_p, cmp_new], axis=1)
        slc_all = jnp.concatenate([slc_p, slc_new], axis=1)
        win_all = jnp.concatenate([win_p, win_new], axis=1)
        win_pos0 = pos0 - win_p.shape[1]
        win_state = win_all[:, -win_p.shape[1]:]
    o_a = mla_attend(cq, ckv_all, kpe_all, q_pos, lw['q_norm'], lw['w_uq'], lw['w_uk'], lw['w_uv'])
    y_a = (o_a * jax.nn.silu(g_mla)) @ lw['w_o_mla']
    gates = jax.nn.sigmoid(br_gate).reshape(B, T, NSA_HEADS, 3)
    o_b = nsa_attend(q_nsa.reshape(B, T, NSA_HEADS, NSA_HD), gates, q_pos, cmp_all, slc_all, win_all,
                     win_pos0, lw['cmp_w1'], lw['cmp_w2'], lw['cmp_pe'], rel_table)
    y_b = (o_b * jax.nn.silu(g_nsa)) @ lw['w_o_nsa']
    m_a, m_b = jnp.split(jax.nn.sigmoid(merge), 2, axis=-1)
    out = (m_a * y_a + m_b * y_b) @ lw['w_out']
    return x + gate[:, None] * out, (ckv, kpe, cmp_new, slc_new, win_state)


def setup_inputs(seed: int = 0) -> dict:
    key = jax.random.key(seed)
    ks = iter(jax.random.split(key, 40))
    nrm = lambda shape, s=1.0: jax.random.normal(next(ks), shape, jnp.float32) * s
    n_pages = PAST_LEN // PAGE_SIZE
    n_used = DEC_BATCH * n_pages
    n_pool = n_used + n_used // 4
    win_buf = min(WINDOW, PAST_LEN)
    page_table = jax.random.permutation(next(ks), n_pool)[:n_used].reshape(DEC_BATCH, n_pages).astype(jnp.int32)
    kv_tail = (2, NSA_GROUPS, NSA_HD)
    return {
        'x_prompt': nrm((BATCH, SEQ, D_MODEL)),
        'x_sample': nrm((DEC_BATCH, DEC_SEQ, D_MODEL)),
        'c_prompt': nrm((BATCH, D_MODEL)),
        'c_sample': nrm((DEC_BATCH, D_MODEL)),
        'cache_mla_ckv': nrm((n_pool, DEPTH, PAGE_SIZE, MLA_KV_RANK)),
        'cache_mla_kpe': nrm((n_pool, DEPTH, PAGE_SIZE, MLA_ROPE)),
        'cache_nsa_cmp': nrm((n_pool, DEPTH, PAGE_SIZE) + kv_tail),
        'cache_nsa_slc': nrm((n_pool, DEPTH, PAGE_SIZE) + kv_tail),
        'state_nsa_win': nrm((DEPTH, DEC_BATCH, win_buf) + kv_tail),
        'page_table': page_table,
        'norm_g': 1.0 + nrm((DEPTH, D_MODEL), 0.02),
        'w_ada': nrm((DEPTH, D_MODEL, 3 * D_MODEL), 0.2 * D_MODEL ** -0.5),
        'b_ada': nrm((DEPTH, 3 * D_MODEL), 0.02),
        'w_in': nrm((DEPTH, D_MODEL, IN_WIDTH), D_MODEL ** -0.5),
        'mla_q_norm': 1.0 + nrm((DEPTH, MLA_Q_RANK), 0.02),
        'mla_kv_norm': 1.0 + nrm((DEPTH, MLA_KV_RANK), 0.02),
        'mla_w_uq': nrm((DEPTH, MLA_Q_RANK, MLA_HEADS * (MLA_NOPE + MLA_ROPE)), MLA_Q_RANK ** -0.5),
        'mla_w_uk': nrm((DEPTH, MLA_KV_RANK, MLA_HEADS, MLA_NOPE), MLA_KV_RANK ** -0.5),
        'mla_w_uv': nrm((DEPTH, MLA_KV_RANK, MLA_HEADS, MLA_VDIM), MLA_KV_RANK ** -0.5),
        'nsa_cmp_w1': nrm((DEPTH, 2, CMP_BLOCK, NSA_HD, CMP_HIDDEN), (CMP_BLOCK * NSA_HD) ** -0.5),
        'nsa_cmp_w2': nrm((DEPTH, 2, CMP_HIDDEN, NSA_HD), CMP_HIDDEN ** -0.5),
        'nsa_cmp_pe': nrm((DEPTH, 2, CMP_BLOCK, NSA_HD), 0.5),
        'w_o_mla': nrm((DEPTH, MLA_WIDTH, D_MODEL), MLA_WIDTH ** -0.5),
        'w_o_nsa': nrm((DEPTH, NSA_WIDTH, D_MODEL), NSA_WIDTH ** -0.5),
        'w_out': nrm((DEPTH, D_MODEL, D_MODEL), D_MODEL ** -0.5),
        'rel_bias': nrm((REL_BUCKETS, NSA_HEADS), 0.5),
        'final_norm': 1.0 + nrm((D_MODEL,), 0.02),
    }


def reference(x_prompt, x_sample, c_prompt, c_sample, cache_mla_ckv, cache_mla_kpe, cache_nsa_cmp,
              cache_nsa_slc, state_nsa_win, page_table, norm_g, w_ada, b_ada, w_in, mla_q_norm,
              mla_kv_norm, mla_w_uq, mla_w_uk, mla_w_uv, nsa_cmp_w1, nsa_cmp_w2, nsa_cmp_pe,
              w_o_mla, w_o_nsa, w_out, rel_bias, final_norm):
    n_pages = PAST_LEN // PAGE_SIZE

    def gather_pages(cache, l):
        rows = cache[page_table, l]
        return rows.reshape((rows.shape[0], n_pages * PAGE_SIZE) + rows.shape[3:])

    xp, xs = x_prompt, x_sample
    rows_p, rows_s = [], []
    for l in range(DEPTH):
        lw = dict(norm=norm_g[l], w_ada=w_ada[l], b_ada=b_ada[l], w_in=w_in[l],
                  q_norm=mla_q_norm[l], kv_norm=mla_kv_norm[l], w_uq=mla_w_uq[l],
                  w_uk=mla_w_uk[l], w_uv=mla_w_uv[l], cmp_w1=nsa_cmp_w1[l], cmp_w2=nsa_cmp_w2[l],
                  cmp_pe=nsa_cmp_pe[l], w_o_mla=w_o_mla[l], w_o_nsa=w_o_nsa[l], w_out=w_out[l])
        xp, rp = layer_forward(xp, c_prompt, 0, None, lw, rel_bias)
        past = (gather_pages(cache_mla_ckv, l), gather_pages(cache_mla_kpe, l),
                gather_pages(cache_nsa_cmp, l), gather_pages(cache_nsa_slc, l), state_nsa_win[l])
        xs, rs = layer_forward(xs, c_sample, PAST_LEN, past, lw, rel_bias)
        rows_p.append(rp)
        rows_s.append(rs)
    p_ckv, p_kpe, p_cmp, p_slc, p_win = [jnp.stack(r) for r in zip(*rows_p)]
    s_ckv, s_kpe, s_cmp, s_slc, s_win = [jnp.stack(r) for r in zip(*rows_s)]
    y_prompt = rmsnorm(xp, final_norm)
    y_sample = rmsnorm(xs, final_norm)
    return (y_prompt, y_sample, p_ckv, p_kpe, p_cmp, p_slc, p_win, s_ckv, s_kpe, s_cmp, s_slc, s_win)
```

```python
import functools
import math

import jax
import jax.numpy as jnp
from jax import lax
from jax.experimental import pallas as pl
from jax.experimental.pallas import tpu as pltpu

D_MODEL = 1024
PAGE_SIZE = 128
MLA_HEADS = 8
MLA_NOPE = 64
MLA_ROPE = 32
MLA_VDIM = 64
MLA_Q_RANK = 256
MLA_KV_RANK = 256
MLA_WIDTH = MLA_HEADS * MLA_VDIM
ROPE_THETA = 10000.0
NSA_HEADS = 8
NSA_GROUPS = 2
NSA_HPG = NSA_HEADS // NSA_GROUPS
NSA_HD = 64
NSA_WIDTH = NSA_HEADS * NSA_HD
CMP_BLOCK = 32
CMP_STRIDE = 16
CMP_HIDDEN = 64
SLC_BLOCK = 64
N_SELECT = 16
WINDOW = 512
REL_BUCKETS = 32
REL_MAX_DIST = 1024
NORM_EPS = 1e-6
MASK_NEG = -1e30
FORCE_BONUS = 1000.0

IN_SPLITS = (MLA_Q_RANK, MLA_KV_RANK, MLA_ROPE, MLA_WIDTH, NSA_WIDTH, 6 * NSA_GROUPS * NSA_HD,
             3 * NSA_HEADS, NSA_WIDTH, 2 * D_MODEL)
IN_OFFSETS = tuple(sum(IN_SPLITS[:i + 1]) for i in range(len(IN_SPLITS) - 1))

KV_ROW = 2 * NSA_GROUPS * NSA_HD
NSA_KV_WIDTH = 3 * KV_ROW
W_IN_MAIN = 4864
W_IN_PAD = 4992
MLA_SCALE = (MLA_NOPE + MLA_ROPE) ** -0.5
NSA_SCALE = NSA_HD ** -0.5
BIAS_DIST = 1152
TQ = 128
LANES = 128
VMEM_LIMIT_BYTES = 56 * 1024 * 1024

F32 = jnp.float32
BF16 = jnp.bfloat16


def _cparams(*sem):
    return pltpu.CompilerParams(dimension_semantics=sem, vmem_limit_bytes=VMEM_LIMIT_BYTES)


def _dot(a, b):
    return jnp.dot(a, b, preferred_element_type=F32)


def _dot_nt(a, b):
    return lax.dot_general(a, b, (((1,), (1,)), ((), ())), preferred_element_type=F32)


def _silu(x):
    return x * jax.nn.sigmoid(x)


def _rms(x, g):
    return x * lax.rsqrt(jnp.mean(x * x, axis=-1, keepdims=True) + NORM_EPS) * g


def _full_spec(shape):
    n = len(shape)
    return pl.BlockSpec(shape, lambda *_: (0,) * n)


def _ada_kernel(c_ref, w_ref, b_ref, o_ref):
    a = _silu(c_ref[...]).astype(BF16)
    o_ref[...] = _dot(a, w_ref[...].astype(BF16)) + b_ref[...]


def _ada(c, w_ada, b_ada):
    depth, d, n = w_ada.shape
    rows = c.shape[0]
    tn = 768
    return pl.pallas_call(
        _ada_kernel,
        grid=(depth, n // tn),
        in_specs=[pl.BlockSpec((rows, d), lambda l, j: (0, 0)),
                  pl.BlockSpec((None, d, tn), lambda l, j: (l, 0, j)),
                  pl.BlockSpec((None, 1, tn), lambda l, j: (l, 0, j))],
        out_specs=pl.BlockSpec((None, rows, tn), lambda l, j: (l, 0, j)),
        out_shape=jax.ShapeDtypeStruct((depth, rows, n), F32),
        compiler_params=_cparams("parallel", "parallel"),
        name="ada_mod",
    )(c, w_ada, b_ada.reshape(depth, 1, n))


def _inproj_kernel(x_ref, g_ref, sc_ref, sh_ref, kvg_ref, cos_ref, sin_ref, w_ref,
                   cq_ref, ckv_ref, ckvb_ref, kpe_ref, kpeb_ref, gmla_ref, qn_ref,
                   kvn_ref, kvnb_ref, br_ref, gnsa_ref, mrg_ref):
    h = (_rms(x_ref[...], g_ref[...]) * (1.0 + sc_ref[...]) + sh_ref[...]).astype(BF16)

    def seg(c0, c1):
        return _dot(h, w_ref[:, c0:c1])

    cq_ref[...] = seg(0, 256)
    ckv = _rms(seg(256, 512), kvg_ref[...])
    ckv_ref[...] = ckv
    ckvb_ref[...] = ckv.astype(BF16)
    gmla_ref[...] = seg(512, 1024)
    qn_ref[...] = (seg(1024, 1536) * NSA_SCALE).astype(BF16)
    kvn = seg(1536, 2304)
    kvn_ref[...] = kvn
    kvnb_ref[...] = kvn.astype(BF16)
    gnsa_ref[...] = seg(2304, 2816)
    for j in range(4):
        mrg_ref[:, 512 * j:512 * (j + 1)] = seg(2816 + 512 * j, 2816 + 512 * (j + 1))
    tail = seg(W_IN_MAIN, W_IN_PAD)
    kpe = tail[:, 0:32] * cos_ref[...] + tail[:, 32:64] * sin_ref[...]
    kpe_ref[...] = kpe
    kpeb_ref[...] = kpe.astype(BF16)
    br_ref[...] = jax.nn.sigmoid(tail[:, 64:88])


def _inproj(x, g, sc, sh, kvg, cos2, sin2, w, per_row):
    m, d = x.shape
    tm = min(256, m)
    nt = m // tm
    nrt = cos2.shape[0] // tm
    if per_row:
        mod_spec = pl.BlockSpec((tm, d), lambda i: (i, 0))
    else:
        tpb = nt // sc.shape[0]
        mod_spec = pl.BlockSpec((None, 1, d), lambda i: (i // tpb, 0, 0))
    row = lambda wd: pl.BlockSpec((tm, wd), lambda i: (i, 0))
    outs = [(256, F32), (256, F32), (256, BF16), (32, F32), (32, BF16), (512, F32), (512, BF16),
            (NSA_KV_WIDTH, F32), (NSA_KV_WIDTH, BF16), (24, F32), (512, F32), (2 * D_MODEL, F32)]
    return pl.pallas_call(
        _inproj_kernel,
        grid=(nt,),
        in_specs=[row(d), _full_spec((1, d)), mod_spec, mod_spec, _full_spec((1, 256)),
                  pl.BlockSpec((tm, 32), lambda i: (i % nrt, 0)),
                  pl.BlockSpec((tm, 32), lambda i: (i % nrt, 0)),
                  _full_spec((d, W_IN_PAD))],
        out_specs=[row(wd) for wd, _ in outs],
        out_shape=[jax.ShapeDtypeStruct((m, wd), dt) for wd, dt in outs],
        compiler_params=_cparams("parallel"),
        name="in_proj",
    )(x, g, sc, sh, kvg, cos2, sin2, w)


def _qprep_kernel(cq_ref, qg_ref, cos_ref, sin_ref, wuq_ref, wuk_ref, qlat_ref, qpe_ref):
    cqn = _rms(cq_ref[...], qg_ref[...]).astype(BF16)
    qn = _dot(cqn, wuq_ref[:, 0:512])
    pe = _dot(cqn, wuq_ref[:, 512:768])
    pes = _dot(cqn, wuq_ref[:, 768:1024])
    qpe = (pe * cos_ref[...] + pes * sin_ref[...]) * MLA_SCALE
    for h in range(MLA_HEADS):
        qh = qn[:, MLA_NOPE * h:MLA_NOPE * (h + 1)].astype(BF16)
        qlat_ref[h] = (_dot(qh, wuk_ref[h]) * MLA_SCALE).astype(BF16)
        qpe_ref[h] = qpe[:, MLA_ROPE * h:MLA_ROPE * (h + 1)].astype(BF16)


def _qprep(cq, qg, cos8, sin8, wuq, wukt):
    m = cq.shape[0]
    tm = min(256, m)
    nrt = cos8.shape[0] // tm
    return pl.pallas_call(
        _qprep_kernel,
        grid=(m // tm,),
        in_specs=[pl.BlockSpec((tm, 256), lambda i: (i, 0)), _full_spec((1, 256)),
                  pl.BlockSpec((tm, 256), lambda i: (i % nrt, 0)),
                  pl.BlockSpec((tm, 256), lambda i: (i % nrt, 0)),
                  _full_spec((256, 1024)), _full_spec((MLA_HEADS, MLA_NOPE, MLA_KV_RANK))],
        out_specs=[pl.BlockSpec((MLA_HEADS, tm, MLA_KV_RANK), lambda i: (0, i, 0)),
                   pl.BlockSpec((MLA_HEADS, tm, MLA_ROPE), lambda i: (0, i, 0))],
        out_shape=[jax.ShapeDtypeStruct((MLA_HEADS, m, MLA_KV_RANK), BF16),
                   jax.ShapeDtypeStruct((MLA_HEADS, m, MLA_ROPE), BF16)],
        compiler_params=_cparams("parallel"),
        name="mla_qprep",
    )(cq, qg, cos8, sin8, wuq, wukt)


def _mla_prompt_kernel(qlat_ref, qpe_ref, ckv_ref, kpe_ref, wuv_ref, o_ref, m_ref, l_ref, acc_ref,
                       *, tq, tk):
    qi = pl.program_id(1)
    rows = MLA_HEADS * tq
    q1 = qlat_ref[...].reshape(rows, MLA_KV_RANK)
    q2 = qpe_ref[...].reshape(rows, MLA_ROPE)
    row_pos = qi * tq + lax.broadcasted_iota(jnp.int32, (rows, tk), 0) % tq
    col = lax.broadcasted_iota(jnp.int32, (rows, tk), 1)
    m_ref[...] = jnp.full((rows, 1), MASK_NEG, F32)
    l_ref[...] = jnp.zeros((rows, 1), F32)
    acc_ref[...] = jnp.zeros((rows, MLA_KV_RANK), F32)

    def body(j, carry):
        k1 = ckv_ref[pl.ds(j * tk, tk), :]
        k2 = kpe_ref[pl.ds(j * tk, tk), :]
        mask = col + j * tk <= row_pos
        s = jnp.where(mask, _dot_nt(q1, k1) + _dot_nt(q2, k2), MASK_NEG)
        m_old = m_ref[...]
        m_new = jnp.maximum(m_old, jnp.max(s, axis=-1, keepdims=True))
        a = jnp.exp(m_old - m_new)
        e = jnp.where(mask, jnp.exp(s - m_new), 0.0)
        l_ref[...] = a * l_ref[...] + jnp.sum(e, axis=-1, keepdims=True)
        acc_ref[...] = a * acc_ref[...] + _dot(e.astype(BF16), k1)
        m_ref[...] = m_new
        return carry

    lax.fori_loop(0, (qi * tq + tq + tk - 1) // tk, body, 0)
    o = acc_ref[...] / jnp.maximum(l_ref[...], 1e-30)
    for h in range(MLA_HEADS):
        o_ref[:, MLA_VDIM * h:MLA_VDIM * (h + 1)] = _dot(o[h * tq:(h + 1) * tq].astype(BF16), wuv_ref[h])


def _mla_prompt(qlat, qpe, ckvb, kpeb, wuv, nb, t):
    tq = min(TQ, t)
    tk = min(256, t)
    nq = t // tq
    rows = MLA_HEADS * tq
    return pl.pallas_call(
        functools.partial(_mla_prompt_kernel, tq=tq, tk=tk),
        grid=(nb, nq),
        in_specs=[pl.BlockSpec((MLA_HEADS, tq, MLA_KV_RANK), lambda b, q: (0, b * nq + q, 0)),
                  pl.BlockSpec((MLA_HEADS, tq, MLA_ROPE), lambda b, q: (0, b * nq + q, 0)),
                  pl.BlockSpec((t, MLA_KV_RANK), lambda b, q: (b, 0)),
                  pl.BlockSpec((t, MLA_ROPE), lambda b, q: (b, 0)),
                  _full_spec((MLA_HEADS, MLA_KV_RANK, MLA_VDIM))],
        out_specs=pl.BlockSpec((tq, MLA_WIDTH), lambda b, q: (b * nq + q, 0)),
        out_shape=jax.ShapeDtypeStruct((nb * t, MLA_WIDTH), F32),
        scratch_shapes=[pltpu.VMEM((rows, 1), F32), pltpu.VMEM((rows, 1), F32),
                        pltpu.VMEM((rows, MLA_KV_RANK), F32)],
        compiler_params=_cparams("parallel", "parallel"),
        name="mla_prompt_attn",
    )(qlat, qpe, ckvb, kpeb, wuv)


def _mla_decode_kernel(pt_ref, *refs, npg, n_tok):
    q1_ref, q2_ref = refs[0], refs[1]
    ckv_refs = refs[2:2 + npg]
    kpe_refs = refs[2 + npg:2 + 2 * npg]
    nckv_ref, nkpe_ref, o_ref, m_ref, l_ref, acc_ref = refs[2 + 2 * npg:]
    step = pl.program_id(1)
    rows = q1_ref.shape[0]

    @pl.when(step == 0)
    def _():
        m_ref[...] = jnp.full((rows, 1), MASK_NEG, F32)
        l_ref[...] = jnp.zeros((rows, 1), F32)
        acc_ref[...] = jnp.zeros((rows, MLA_KV_RANK), F32)

    q1 = q1_ref[...]
    q2 = q2_ref[...]
    ks = [r[...].astype(BF16) for r in ckv_refs]
    ss = [_dot_nt(q1, k) + _dot_nt(q2, r[...].astype(BF16)) for k, r in zip(ks, kpe_refs)]
    m_old = m_ref[...]
    m_new = m_old
    for s in ss:
        m_new = jnp.maximum(m_new, jnp.max(s, axis=-1, keepdims=True))
    a = jnp.exp(m_old - m_new)
    l_new = a * l_ref[...]
    acc = a * acc_ref[...]
    for s, k in zip(ss, ks):
        e = jnp.exp(s - m_new)
        l_new = l_new + jnp.sum(e, axis=-1, keepdims=True)
        acc = acc + _dot(e.astype(BF16), k)
    m_ref[...] = m_new
    l_ref[...] = l_new
    acc_ref[...] = acc

    @pl.when(step == pl.num_programs(1) - 1)
    def _():
        kn = nckv_ref[...].astype(BF16)
        s = _dot_nt(q1, kn) + _dot_nt(q2, nkpe_ref[...].astype(BF16))
        tok = lax.broadcasted_iota(jnp.int32, s.shape, 0) % n_tok
        key = lax.broadcasted_iota(jnp.int32, s.shape, 1)
        mask = key <= tok
        s = jnp.where(mask, s, MASK_NEG)
        m_old = m_ref[...]
        m_new = jnp.maximum(m_old, jnp.max(s, axis=-1, keepdims=True))
        a = jnp.exp(m_old - m_new)
        e = jnp.where(mask, jnp.exp(s - m_new), 0.0)
        l_fin = a * l_ref[...] + jnp.sum(e, axis=-1, keepdims=True)
        acc = a * acc_ref[...] + _dot(e.astype(BF16), kn)
        o_ref[...] = acc / jnp.maximum(l_fin, 1e-30)


def _mla_decode(page_tbl, qd, qped, cache_ckv, cache_kpe, new_ckv, new_kpe, layer, n_tok):
    nb, rows, _ = qd.shape
    n_pages = page_tbl.shape[0] // nb
    npg = min(16, n_pages)
    steps = n_pages // npg

    def page_map(p):
        return lambda b, s, pt: (pt[b * n_pages + s * npg + p], layer, 0, 0)

    in_specs = [pl.BlockSpec((None, rows, MLA_KV_RANK), lambda b, s, pt: (b, 0, 0)),
                pl.BlockSpec((None, rows, MLA_ROPE), lambda b, s, pt: (b, 0, 0))]
    in_specs += [pl.BlockSpec((None, None, PAGE_SIZE, MLA_KV_RANK), page_map(p)) for p in range(npg)]
    in_specs += [pl.BlockSpec((None, None, PAGE_SIZE, MLA_ROPE), page_map(p)) for p in range(npg)]
    in_specs += [pl.BlockSpec((None, 16, MLA_KV_RANK), lambda b, s, pt: (b, 0, 0)),
                 pl.BlockSpec((None, 16, MLA_ROPE), lambda b, s, pt: (b, 0, 0))]
    return pl.pallas_call(
        functools.partial(_mla_decode_kernel, npg=npg, n_tok=n_tok),
        grid_spec=pltpu.PrefetchScalarGridSpec(
            num_scalar_prefetch=1, grid=(nb, steps), in_specs=in_specs,
            out_specs=pl.BlockSpec((None, rows, MLA_KV_RANK), lambda b, s, pt: (b, 0, 0)),
            scratch_shapes=[pltpu.VMEM((rows, 1), F32), pltpu.VMEM((rows, 1), F32),
                            pltpu.VMEM((rows, MLA_KV_RANK), F32)]),
        out_shape=jax.ShapeDtypeStruct((nb, rows, MLA_KV_RANK), F32),
        compiler_params=_cparams("parallel", "arbitrary"),
        name="mla_decode_attn",
    )(page_tbl, qd, qped, *([cache_ckv] * npg), *([cache_kpe] * npg), new_ckv, new_kpe)


def _uv_kernel(o_ref, wuv_ref, out_ref):
    for h in range(MLA_HEADS):
        out_ref[:, MLA_VDIM * h:MLA_VDIM * (h + 1)] = _dot(o_ref[h].astype(BF16), wuv_ref[h])


def _uv(olat, wuv):
    m = olat.shape[1]
    return pl.pallas_call(
        _uv_kernel,
        grid=(1,),
        in_specs=[_full_spec(olat.shape), _full_spec(wuv.shape)],
        out_specs=_full_spec((m, MLA_WIDTH)),
        out_shape=jax.ShapeDtypeStruct((m, MLA_WIDTH), F32),
        compiler_params=_cparams("arbitrary"),
        name="mla_decode_uv",
    )(olat, wuv)


def _compress_kernel(*refs, n_src, chunks_per_src):
    refs = refs[len(refs) - 2 * n_src - 4:]
    src_l, src_r = refs[:n_src], refs[n_src:2 * n_src]
    pe_ref, wlo_ref, whi_ref, o_ref = refs[2 * n_src:]
    c = n_src * chunks_per_src
    acc_a = jnp.zeros((c + 16, KV_ROW), F32)
    acc_b = jnp.zeros((c + 16, KV_ROW), F32)
    for i in range(CMP_STRIDE):
        rows = pl.ds(i, chunks_per_src, stride=CMP_STRIDE)
        slab = [jnp.concatenate([a[rows, :], b[rows, :]], axis=1) for a, b in zip(src_l, src_r)]
        lo = jnp.concatenate(slab + [pe_ref[i]], axis=0).astype(BF16)
        hi = jnp.concatenate(slab + [pe_ref[CMP_STRIDE + i]], axis=0).astype(BF16)
        acc_a = acc_a + _dot(lo, wlo_ref[i])
        acc_b = acc_b + _dot(hi, whi_ref[i])
    pe_term = acc_a[c:c + 1] + acc_b[c:c + 1]
    o_ref[:, 0:KV_ROW] = acc_a[:c] + pe_term
    o_ref[:, KV_ROW:2 * KV_ROW] = acc_b[:c]


def _compress_prompt(kvn, pe16, wlo, whi, nb, t):
    nch = t // CMP_STRIDE
    return pl.pallas_call(
        functools.partial(_compress_kernel, n_src=1, chunks_per_src=nch),
        grid=(nb,),
        in_specs=[pl.BlockSpec((t, LANES), lambda b: (b, 0)), pl.BlockSpec((t, LANES), lambda b: (b, 1)),
                  _full_spec(pe16.shape), _full_spec(wlo.shape), _full_spec(whi.shape)],
        out_specs=pl.BlockSpec((None, nch, 2 * KV_ROW), lambda b: (b, 0, 0)),
        out_shape=jax.ShapeDtypeStruct((nb, nch, 2 * KV_ROW), F32),
        compiler_params=_cparams("parallel"),
        name="nsa_compress_prompt",
    )(kvn, kvn, pe16, wlo, whi)


def _compress_decode(page_tbl, cache_cmp, pe16, wlo, whi, nb):
    depth = cache_cmp.shape[1]
    n_pages = page_tbl.shape[0] // nb
    npg = min(16, n_pages)
    steps = n_pages // npg
    cpp = PAGE_SIZE // CMP_STRIDE

    def page_map(p, half):
        return lambda l, b, s, pt: (pt[b * n_pages + s * npg + p], l, 0, half)

    wspec = pl.BlockSpec((None, CMP_STRIDE, KV_ROW, KV_ROW), lambda l, b, s, pt: (l, 0, 0, 0))
    in_specs = [pl.BlockSpec((None, None, PAGE_SIZE, LANES), page_map(p, half))
                for half in range(2) for p in range(npg)]
    in_specs += [pl.BlockSpec((None, CMP_BLOCK, 16, KV_ROW), lambda l, b, s, pt: (l, 0, 0, 0)), wspec, wspec]
    return pl.pallas_call(
        functools.partial(_compress_kernel, n_src=npg, chunks_per_src=cpp),
        grid_spec=pltpu.PrefetchScalarGridSpec(
            num_scalar_prefetch=1, grid=(depth, nb, steps), in_specs=in_specs,
            out_specs=pl.BlockSpec((None, None, npg * cpp, 2 * KV_ROW), lambda l, b, s, pt: (l, b, s, 0))),
        out_shape=jax.ShapeDtypeStruct((depth, nb, n_pages * cpp, 2 * KV_ROW), F32),
        compiler_params=_cparams("parallel", "parallel", "arbitrary"),
        name="nsa_compress_decode",
    )(page_tbl, *([cache_cmp] * (2 * npg)), pe16, wlo, whi)


def _compress_finish(ab_ref, w2_ref):
    n = ab_ref.shape[0]
    h = ab_ref[:, 0:KV_ROW] + pltpu.roll(ab_ref[:, KV_ROW:2 * KV_ROW], n - 1, 0)
    return _dot(_silu(h).astype(BF16), w2_ref[...]).astype(BF16)


def _softmax_rows(s, mask):
    s = jnp.where(mask, s, MASK_NEG)
    e = jnp.where(mask, jnp.exp(s - jnp.max(s, axis=-1, keepdims=True)), 0.0)
    return e / jnp.maximum(jnp.sum(e, axis=-1, keepdims=True), 1e-30)


def _pool_exact(x, pool):
    hi = x.astype(BF16)
    r1 = x - hi.astype(F32)
    mid = r1.astype(BF16)
    lo = (r1 - mid.astype(F32)).astype(BF16)
    return _dot(hi, pool) + _dot(mid, pool) + _dot(lo, pool)


def _nsa_prompt_kernel(qn_ref, kvn_ref, ab_ref, gates_ref, biasc_ref, tz_ref, w2_ref, pool_ref, e_ref,
                       o_ref, ckcv_ref, selm_ref, m_ref, l_ref, acc_ref, *, tq, n_slc):
    qi = pl.program_id(1)
    rows = NSA_HPG * tq
    n_cmp_pad = ckcv_ref.shape[0]

    @pl.when(qi == 0)
    def _():
        ckcv_ref[...] = _compress_finish(ab_ref, w2_ref)

    row_pos = qi * tq + lax.broadcasted_iota(jnp.int32, (rows, tq), 0) % tq
    lane = lax.broadcasted_iota(jnp.int32, (rows, tq), 1)
    gates = gates_ref[...]

    def attend(lo, hi, kcol, vcol, qg, g, mask_fn):
        m_ref[...] = jnp.full((rows, 1), MASK_NEG, F32)
        l_ref[...] = jnp.zeros((rows, 1), F32)
        acc_ref[...] = jnp.zeros((rows, NSA_HD), F32)

        def body(kj, carry):
            k = kvn_ref[pl.ds(kj * tq, tq), kcol:kcol + NSA_HD]
            v = kvn_ref[pl.ds(kj * tq, tq), vcol:vcol + NSA_HD]
            bias = tz_ref[qi - kj, NSA_HPG * g:NSA_HPG * (g + 1)].reshape(rows, tq)
            dist = row_pos - (kj * tq + lane)
            mask = mask_fn(kj, dist)
            s = jnp.where(mask, _dot_nt(qg, k) + bias, MASK_NEG)
            m_old = m_ref[...]
            m_new = jnp.maximum(m_old, jnp.max(s, axis=-1, keepdims=True))
            a = jnp.exp(m_old - m_new)
            e = jnp.where(mask, jnp.exp(s - m_new), 0.0)
            l_ref[...] = a * l_ref[...] + jnp.sum(e, axis=-1, keepdims=True)
            acc_ref[...] = a * acc_ref[...] + _dot(e.astype(BF16), v)
            m_ref[...] = m_new
            return carry

        lax.fori_loop(lo, hi, body, 0)
        return acc_ref[...] / jnp.maximum(l_ref[...], 1e-30)

    for g in range(NSA_GROUPS):
        qg = jnp.concatenate([qn_ref[:, NSA_HD * (NSA_HPG * g + hh):NSA_HD * (NSA_HPG * g + hh + 1)]
                              for hh in range(NSA_HPG)], axis=0)
        ck = ckcv_ref[:, NSA_HD * g:NSA_HD * (g + 1)]
        cv = ckcv_ref[:, 2 * NSA_HD + NSA_HD * g:2 * NSA_HD + NSA_HD * (g + 1)]
        bias_c = biasc_ref[NSA_HPG * g:NSA_HPG * (g + 1)].reshape(rows, n_cmp_pad)
        n_idx = lax.broadcasted_iota(jnp.int32, (rows, n_cmp_pad), 1)
        pos_c = qi * tq + lax.broadcasted_iota(jnp.int32, (rows, n_cmp_pad), 0) % tq
        p_c = _softmax_rows(_dot_nt(qg, ck) + bias_c, n_idx * CMP_STRIDE + CMP_BLOCK - 1 <= pos_c)
        o_c = _dot(p_c.astype(BF16), cv)
        imp = p_c[0:tq]
        for hh in range(1, NSA_HPG):
            imp = imp + p_c[hh * tq:(hh + 1) * tq]
        pooled_t = _pool_exact(imp, pool_ref[...]).T[0:n_slc]
        j_idx = lax.broadcasted_iota(jnp.int32, (n_slc, tq), 0)
        pos_q = qi * tq + lax.broadcasted_iota(jnp.int32, (n_slc, tq), 1)
        cur = pos_q // SLC_BLOCK
        forced = (j_idx == 0) | (j_idx == cur) | (j_idx == cur - 1)
        score = jnp.where(j_idx * SLC_BLOCK <= pos_q, pooled_t + FORCE_BONUS * forced.astype(F32), -1.0)
        rank = jnp.zeros((n_slc, tq), F32)
        for i in range(n_slc):
            si = score[i:i + 1, :]
            rank = rank + ((si > score) | ((si == score) & (i < j_idx))).astype(F32)
        sel_t = (rank < float(min(N_SELECT, n_slc))).astype(F32)
        sel = jnp.concatenate([sel_t, jnp.zeros((LANES - n_slc, tq), F32)], axis=0).T
        sm_all = _dot(sel.astype(BF16), e_ref[...])
        for kb in range(selm_ref.shape[0]):
            selm_ref[kb] = sm_all[:, kb * tq:(kb + 1) * tq]

        def slc_mask(kj, dist):
            sm = selm_ref[kj]
            return (dist >= 0) & (jnp.concatenate([sm] * NSA_HPG, axis=0) > 0.5)

        def win_mask(kj, dist):
            return (dist >= 0) & (dist < WINDOW)

        o_s = attend(0, qi + 1, KV_ROW + NSA_HD * g, KV_ROW + 2 * NSA_HD + NSA_HD * g, qg, g, slc_mask)
        o_w = attend(jnp.maximum(qi - WINDOW // tq, 0), qi + 1,
                     2 * KV_ROW + NSA_HD * g, 2 * KV_ROW + 2 * NSA_HD + NSA_HD * g, qg, g, win_mask)
        for hh in range(NSA_HPG):
            hd = NSA_HPG * g + hh
            sl = slice(hh * tq, (hh + 1) * tq)
            o_ref[:, NSA_HD * hd:NSA_HD * (hd + 1)] = (gates[:, 3 * hd:3 * hd + 1] * o_c[sl]
                                                       + gates[:, 3 * hd + 1:3 * hd + 2] * o_s[sl]
                                                       + gates[:, 3 * hd + 2:3 * hd + 3] * o_w[sl])


def _nsa_prompt(qn, kvnb, ab, gates, biasc, tz, w2bd, pool, emat, nb, t):
    tq = min(TQ, t)
    nq = t // tq
    nch = t // CMP_STRIDE
    n_slc = t // SLC_BLOCK
    rows = NSA_HPG * tq
    return pl.pallas_call(
        functools.partial(_nsa_prompt_kernel, tq=tq, n_slc=n_slc),
        grid=(nb, nq),
        in_specs=[pl.BlockSpec((tq, NSA_WIDTH), lambda b, q: (b * nq + q, 0)),
                  pl.BlockSpec((t, NSA_KV_WIDTH), lambda b, q: (b, 0)),
                  pl.BlockSpec((None, nch, 2 * KV_ROW), lambda b, q: (b, 0, 0)),
                  pl.BlockSpec((tq, 3 * NSA_HEADS), lambda b, q: (b * nq + q, 0)),
                  pl.BlockSpec((NSA_HEADS, tq, nch), lambda b, q: (0, q, 0)),
                  _full_spec(tz.shape), _full_spec(w2bd.shape), _full_spec(pool.shape),
                  _full_spec(emat.shape)],
        out_specs=pl.BlockSpec((tq, NSA_WIDTH), lambda b, q: (b * nq + q, 0)),
        out_shape=jax.ShapeDtypeStruct((nb * t, NSA_WIDTH), F32),
        scratch_shapes=[pltpu.VMEM((nch, KV_ROW), BF16), pltpu.VMEM((nq, tq, tq), F32),
                        pltpu.VMEM((rows, 1), F32), pltpu.VMEM((rows, 1), F32),
                        pltpu.VMEM((rows, NSA_HD), F32)],
        compiler_params=_cparams("parallel", "arbitrary"),
        name="nsa_prompt_attn",
    )(qn, kvnb, ab, gates, biasc, tz, w2bd, pool, emat)


def _nsa_decode_a_kernel(q_ref, ab_ref, win_ref, biasc_ref, biasw_ref, w2_ref, pool_ref,
                         oc_ref, ow_ref, idx_ref, *, past, n_slc, n_win):
    ckcv = _compress_finish(ab_ref, w2_ref)
    nch = ckcv.shape[0]
    for g in range(NSA_GROUPS):
        q = q_ref[g]
        rows = q.shape[0]
        tok = lax.broadcasted_iota(jnp.int32, (rows, nch), 0) % 8
        n_idx = lax.broadcasted_iota(jnp.int32, (rows, nch), 1)
        ck = ckcv[:, NSA_HD * g:NSA_HD * (g + 1)]
        cv = ckcv[:, 2 * NSA_HD + NSA_HD * g:2 * NSA_HD + NSA_HD * (g + 1)]
        valid_c = (n_idx * CMP_STRIDE + CMP_BLOCK - 1 <= past + tok) & (n_idx < nch - 1)
        p_c = _softmax_rows(_dot_nt(q, ck) + biasc_ref[g], valid_c)
        oc_ref[g] = _dot(p_c.astype(BF16), cv)
        imp = p_c[0:8]
        for hh in range(1, NSA_HPG):
            imp = imp + p_c[8 * hh:8 * (hh + 1)]
        pooled = _pool_exact(imp, pool_ref[...])
        nl = pooled.shape[1]
        j_idx = lax.broadcasted_iota(jnp.int32, (8, nl), 1)
        pos = past + lax.broadcasted_iota(jnp.int32, (8, nl), 0)
        cur = pos // SLC_BLOCK
        forced = (j_idx == 0) | (j_idx == cur) | (j_idx == cur - 1)
        score = jnp.where(j_idx * SLC_BLOCK <= pos, pooled + FORCE_BONUS * forced.astype(F32), -1.0)
        score = jnp.where(j_idx < n_slc, score, -2.0)
        k_lane = lax.broadcasted_iota(jnp.int32, (8, LANES), 1)
        picked = jnp.zeros((8, LANES), jnp.int32)
        for k in range(N_SELECT):
            mx = jnp.max(score, axis=-1, keepdims=True)
            ix = jnp.min(jnp.where(score == mx, j_idx, nl), axis=-1, keepdims=True)
            picked = jnp.where(k_lane == k, ix, picked)
            score = jnp.where(j_idx == ix, -3.0, score)
        idx_ref[g] = picked
        wk = win_ref[:, NSA_HD * g:NSA_HD * (g + 1)].astype(BF16)
        wv = win_ref[:, 2 * NSA_HD + NSA_HD * g:2 * NSA_HD + NSA_HD * (g + 1)].astype(BF16)
        nw = wk.shape[0]
        tok_w = lax.broadcasted_iota(jnp.int32, (rows, nw), 0) % 8
        w_idx = lax.broadcasted_iota(jnp.int32, (rows, nw), 1)
        dist = tok_w + (n_win - w_idx)
        valid_w = (dist >= 0) & (dist < WINDOW)
        p_w = _softmax_rows(_dot_nt(q, wk) + biasw_ref[g], valid_w)
        ow_ref[g] = _dot(p_w.astype(BF16), wv)


def _nsa_decode_a(qa, ab, layer, win_all, biasc_d, biasw_d, w2bd, pool_d, past, n_slc, n_win):
    nb, _, rows, _ = qa.shape
    nch = ab.shape[2]
    nwp = win_all.shape[1]
    outs = [(rows, NSA_HD, F32), (rows, NSA_HD, F32), (8, LANES, jnp.int32)]
    specs = [pl.BlockSpec((None, NSA_GROUPS, r, wd), lambda b: (b, 0, 0, 0)) for r, wd, _ in outs]
    shapes = [jax.ShapeDtypeStruct((nb, NSA_GROUPS, r, wd), dt) for r, wd, dt in outs]
    return pl.pallas_call(
        functools.partial(_nsa_decode_a_kernel, past=past, n_slc=n_slc, n_win=n_win),
        grid=(nb,),
        in_specs=[pl.BlockSpec((None, NSA_GROUPS, rows, NSA_HD), lambda b: (b, 0, 0, 0)),
                  pl.BlockSpec((None, None, nch, 2 * KV_ROW), lambda b: (layer, b, 0, 0)),
                  pl.BlockSpec((None, nwp, KV_ROW), lambda b: (b, 0, 0)),
                  _full_spec(biasc_d.shape), _full_spec(biasw_d.shape), _full_spec(w2bd.shape),
                  _full_spec(pool_d.shape)],
        out_specs=specs,
        out_shape=shapes,
        compiler_params=_cparams("parallel"),
        name="nsa_decode_select",
    )(qa, ab, win_all, biasc_d, biasw_d, w2bd, pool_d)


def _nsa_decode_b_kernel(idx_ref, pt_ref, q_ref, *refs, past, n_tok):
    blk_refs = refs[:N_SELECT]
    new_ref, tbl_ref, o_ref = refs[N_SELECT:]
    b, t, g = pl.program_id(0), pl.program_id(1), pl.program_id(2)
    base = ((b * NSA_GROUPS + g) * n_tok + t) * N_SELECT
    last = past // SLC_BLOCK
    q = q_ref[...]
    jj = lax.broadcasted_iota(jnp.int32, (16, SLC_BLOCK), 1)
    ss, ms, vs = [], [], []
    for k in range(N_SELECT):
        ix = idx_ref[base + k]
        blk = jnp.where(ix == last, new_ref[...], blk_refs[k][...]).astype(BF16)
        tb = tbl_ref[t, g, ix // 2]
        bias = jnp.where(ix % 2 == 1, tb[:, SLC_BLOCK:2 * SLC_BLOCK], tb[:, 0:SLC_BLOCK])
        bias = jnp.concatenate([bias, jnp.zeros((8, SLC_BLOCK), F32)], axis=0)
        mask = ix * SLC_BLOCK + jj <= past + t
        ss.append(jnp.where(mask, _dot_nt(q, blk) + bias, MASK_NEG))
        ms.append(mask)
        vs.append(blk)
    mx = jnp.max(ss[0], axis=-1, keepdims=True)
    for s in ss[1:]:
        mx = jnp.maximum(mx, jnp.max(s, axis=-1, keepdims=True))
    es = [jnp.where(m, jnp.exp(s - mx), 0.0) for s, m in zip(ss, ms)]
    den = jnp.sum(es[0], axis=-1, keepdims=True)
    for e in es[1:]:
        den = den + jnp.sum(e, axis=-1, keepdims=True)
    acc = jnp.zeros((16, KV_ROW), F32)
    for e, v in zip(es, vs):
        acc = acc + _dot(e.astype(BF16), v)
    o_ref[...] = acc / jnp.maximum(den, 1e-30)


def _nsa_decode_b(idx_flat, page_tbl, qb, cache_slc, new_slc, tbl_s, layer, past):
    nb, n_tok = qb.shape[0], qb.shape[1]
    n_pages = page_tbl.shape[0] // nb
    last_cached = past // SLC_BLOCK - 1
    per_page = PAGE_SIZE // SLC_BLOCK

    def blk_map(k):
        def f(b, t, g, idx, pt):
            ix = jnp.minimum(idx[((b * NSA_GROUPS + g) * n_tok + t) * N_SELECT + k], last_cached)
            return (pt[b * n_pages + ix // per_page], layer, ix % per_page, 0)
        return f

    in_specs = [pl.BlockSpec((None, None, None, 16, KV_ROW), lambda b, t, g, idx, pt: (b, t, g, 0, 0))]
    in_specs += [pl.BlockSpec((None, None, SLC_BLOCK, KV_ROW), blk_map(k)) for k in range(N_SELECT)]
    in_specs += [pl.BlockSpec((None, SLC_BLOCK, KV_ROW), lambda b, t, g, idx, pt: (b, 0, 0)),
                 pl.BlockSpec(tbl_s.shape, lambda b, t, g, idx, pt: (0,) * 5)]
    return pl.pallas_call(
        functools.partial(_nsa_decode_b_kernel, past=past, n_tok=n_tok),
        grid_spec=pltpu.PrefetchScalarGridSpec(
            num_scalar_prefetch=2, grid=(nb, n_tok, NSA_GROUPS), in_specs=in_specs,
            out_specs=pl.BlockSpec((None, None, None, 16, KV_ROW), lambda b, t, g, idx, pt: (b, t, g, 0, 0))),
        out_shape=jax.ShapeDtypeStruct((nb, n_tok, NSA_GROUPS, 16, KV_ROW), F32),
        compiler_params=_cparams("parallel", "parallel", "parallel"),
        name="nsa_decode_selected",
    )(idx_flat, page_tbl, qb, *([cache_slc] * N_SELECT), new_slc, tbl_s)


def _combine_kernel(oc_ref, os_ref, ow_ref, g_ref, o_ref):
    w = NSA_WIDTH
    g = g_ref[...]
    o_ref[...] = g[:, 0:w] * oc_ref[...] + g[:, w:2 * w] * os_ref[...] + g[:, 2 * w:3 * w] * ow_ref[...]


def _combine(oc, os_, ow, g3):
    m = oc.shape[0]
    return pl.pallas_call(
        _combine_kernel,
        grid=(1,),
        in_specs=[_full_spec(oc.shape), _full_spec(os_.shape), _full_spec(ow.shape), _full_spec(g3.shape)],
        out_specs=_full_spec((m, NSA_WIDTH)),
        out_shape=jax.ShapeDtypeStruct((m, NSA_WIDTH), F32),
        compiler_params=_cparams("arbitrary"),
        name="nsa_decode_combine",
    )(oc, os_, ow, g3)


def _merge_kernel(x_ref, gate_ref, oa_ref, gmla_ref, ob_ref, gnsa_ref, mrg_ref, woa_ref, wob_ref, wout_ref,
                  o_ref):
    ya = _dot((oa_ref[...] * _silu(gmla_ref[...])).astype(BF16), woa_ref[...])
    yb = _dot((ob_ref[...] * _silu(gnsa_ref[...])).astype(BF16), wob_ref[...])
    z = jax.nn.sigmoid(mrg_ref[:, 0:D_MODEL]) * ya + jax.nn.sigmoid(mrg_ref[:, D_MODEL:2 * D_MODEL]) * yb
    o_ref[...] = x_ref[...] + gate_ref[...] * _dot(z.astype(BF16), wout_ref[...])


def _merge(x, gate, oa, gmla, ob, gnsa, mrg, woa, wob, wout, per_row):
    m, d = x.shape
    tm = min(256, m)
    nt = m // tm
    if per_row:
        gate_spec = pl.BlockSpec((tm, d), lambda i: (i, 0))
    else:
        tpb = nt // gate.shape[0]
        gate_spec = pl.BlockSpec((None, 1, d), lambda i: (i // tpb, 0, 0))
    row = lambda wd: pl.BlockSpec((tm, wd), lambda i: (i, 0))
    return pl.pallas_call(
        _merge_kernel,
        grid=(nt,),
        in_specs=[row(d), gate_spec, row(MLA_WIDTH), row(MLA_WIDTH), row(NSA_WIDTH), row(NSA_WIDTH),
                  row(2 * d), _full_spec(woa.shape), _full_spec(wob.shape), _full_spec(wout.shape)],
        out_specs=row(d),
        out_shape=jax.ShapeDtypeStruct((m, d), F32),
        compiler_params=_cparams("parallel"),
        name="merge_out",
    )(x, gate, oa, gmla, ob, gnsa, mrg, woa, wob, wout)


def _final_norm_kernel(x_ref, g_ref, o_ref):
    o_ref[...] = _rms(x_ref[...], g_ref[...])


def _final_norm(x, g):
    m, d = x.shape
    tm = min(512, m)
    return pl.pallas_call(
        _final_norm_kernel,
        grid=(m // tm,),
        in_specs=[pl.BlockSpec((tm, d), lambda i: (i, 0)), _full_spec((1, d))],
        out_specs=pl.BlockSpec((tm, d), lambda i: (i, 0)),
        out_shape=jax.ShapeDtypeStruct((m, d), F32),
        compiler_params=_cparams("parallel"),
        name="final_norm",
    )(x, g)


def _rel_bucket(dist):
    n = jnp.maximum(dist, 0)
    exact = REL_BUCKETS // 2
    nf = jnp.maximum(n, exact).astype(jnp.float32)
    big = exact + (jnp.log(nf / exact) / math.log(REL_MAX_DIST / exact)
                   * (REL_BUCKETS - exact)).astype(jnp.int32)
    return jnp.where(n < exact, n, jnp.minimum(big, REL_BUCKETS - 1))


def _bias_by_dist(rel_bias, dist):
    fd = rel_bias[_rel_bucket(jnp.arange(BIAS_DIST))]
    return jnp.moveaxis(fd[jnp.clip(dist, 0, BIAS_DIST - 1)], -1, 0)


def _rope_tables(pos):
    half = MLA_ROPE // 2
    freq = ROPE_THETA ** (-jnp.arange(half, dtype=jnp.float32) / half)
    ang = pos.astype(jnp.float32)[:, None] * freq[None, :]
    cos, sin = jnp.cos(ang), jnp.sin(ang)
    return jnp.concatenate([cos, cos], axis=-1), jnp.concatenate([-sin, sin], axis=-1)


def _block_diag4(w):
    blocks = [w[0], w[0], w[1], w[1]]
    z = jnp.zeros_like(w[0])
    rows = [jnp.concatenate([blk if i == c else z for i in range(4)], axis=-1) for c, blk in enumerate(blocks)]
    return jnp.concatenate(rows, axis=-2)


def _prep_weights(w_in, mla_w_uq, mla_w_uk, mla_w_uv, nsa_cmp_w1, nsa_cmp_w2, nsa_cmp_pe):
    depth = w_in.shape[0]
    swap = jnp.concatenate([jnp.arange(MLA_ROPE // 2, MLA_ROPE), jnp.arange(0, MLA_ROPE // 2)])
    cq, ckv, kpe, g_mla, q_nsa, kv_nsa, br, g_nsa, merge = jnp.split(w_in, IN_OFFSETS, axis=-1)
    pad = jnp.zeros((depth, D_MODEL, W_IN_PAD - W_IN_MAIN - 2 * MLA_ROPE - 3 * NSA_HEADS), w_in.dtype)
    w_in_p = jnp.concatenate([cq, ckv, g_mla, q_nsa, kv_nsa, g_nsa, merge, kpe, kpe[..., swap], br, pad],
                             axis=-1).astype(BF16)
    uq = mla_w_uq.reshape(depth, MLA_Q_RANK, MLA_HEADS, MLA_NOPE + MLA_ROPE)
    uq_pe = uq[..., MLA_NOPE:]
    w_uq_p = jnp.concatenate([uq[..., :MLA_NOPE].reshape(depth, MLA_Q_RANK, -1),
                              uq_pe.reshape(depth, MLA_Q_RANK, -1),
                              uq_pe[..., swap].reshape(depth, MLA_Q_RANK, -1)], axis=-1).astype(BF16)
    w_ukt = jnp.transpose(mla_w_uk, (0, 2, 3, 1)).astype(BF16)
    w_uv = jnp.transpose(mla_w_uv, (0, 2, 1, 3)).astype(BF16)
    w1 = jnp.moveaxis(nsa_cmp_w1, 1, 0)
    w1bd = _block_diag4(w1).astype(BF16)
    w2bd = _block_diag4(jnp.moveaxis(nsa_cmp_w2, 1, 0)).astype(BF16)
    pe = jnp.moveaxis(nsa_cmp_pe, 1, 0)
    pe_cat = jnp.concatenate([pe[0], pe[0], pe[1], pe[1]], axis=-1)
    pe16 = jnp.broadcast_to(pe_cat[:, :, None, :], (depth, CMP_BLOCK, 16, KV_ROW))
    return w_in_p, w_uq_p, w_ukt, w_uv, w1bd[:, :CMP_STRIDE], w1bd[:, CMP_STRIDE:], w2bd, pe16


def kernel(x_prompt, x_sample, c_prompt, c_sample, cache_mla_ckv, cache_mla_kpe, cache_nsa_cmp, cache_nsa_slc, state_nsa_win, page_table, norm_g, w_ada, b_ada, w_in, mla_q_norm, mla_kv_norm, mla_w_uq, mla_w_uk, mla_w_uv, nsa_cmp_w1, nsa_cmp_w2, nsa_cmp_pe, w_o_mla, w_o_nsa, w_out, rel_bias, final_norm):
    nb, t, d = x_prompt.shape
    db, n_tok, _ = x_sample.shape
    depth = w_in.shape[0]
    n_pages = page_table.shape[1]
    past = n_pages * PAGE_SIZE
    n_win = state_nsa_win.shape[2]
    mp, ms = nb * t, db * n_tok
    kv_tail = (2, NSA_GROUPS, NSA_HD)

    (w_in_p, w_uq_p, w_ukt, w_uv, w1lo, w1hi, w2bd, pe16) = _prep_weights(
        w_in, mla_w_uq, mla_w_uk, mla_w_uv, nsa_cmp_w1, nsa_cmp_w2, nsa_cmp_pe)
    w_oa, w_ob, w_o = w_o_mla.astype(BF16), w_o_nsa.astype(BF16), w_out.astype(BF16)
    cache_cmp = cache_nsa_cmp.reshape(cache_nsa_cmp.shape[:3] + (KV_ROW,))
    cache_slc = cache_nsa_slc.reshape(cache_nsa_slc.shape[:3] + (KV_ROW,))
    page_flat = page_table.reshape(-1)

    cos_p, sin_p = _rope_tables(jnp.arange(t))
    cos_s, sin_s = _rope_tables(past + jnp.arange(n_tok))
    cos_s, sin_s = jnp.tile(cos_s, (db, 1)), jnp.tile(sin_s, (db, 1))
    tile8 = lambda a: jnp.tile(a, (1, MLA_HEADS))
    tq = min(TQ, t)
    nq = t // tq
    nch_p = t // CMP_STRIDE
    n_slc_p = t // SLC_BLOCK
    ii = jnp.arange(tq)
    tz = _bias_by_dist(rel_bias, (jnp.arange(nq) * tq)[:, None, None] + ii[None, :, None] - ii[None, None, :])
    tz = jnp.moveaxis(tz, 0, 1)
    biasc_p = _bias_by_dist(rel_bias, jnp.arange(t)[:, None]
                            - (jnp.arange(nch_p) * CMP_STRIDE + CMP_BLOCK - 1)[None, :])
    pool_p = (jnp.arange(nch_p)[:, None] // (SLC_BLOCK // CMP_STRIDE) == jnp.arange(LANES)[None, :]).astype(BF16)
    emat_p = (jnp.arange(LANES)[:, None] == jnp.arange(t)[None, :] // SLC_BLOCK).astype(BF16)

    nch_s = past // CMP_STRIDE
    n_slc_s = past // SLC_BLOCK + 1
    nsl_pad = -(-n_slc_s // LANES) * LANES
    tok8 = jnp.arange(8)
    bc = _bias_by_dist(rel_bias, past + tok8[:, None] - (jnp.arange(nch_s) * CMP_STRIDE + CMP_BLOCK - 1)[None, :])
    biasc_s = bc.reshape(NSA_GROUPS, NSA_HPG * 8, nch_s)
    nwp = -(-(n_win + 8) // LANES) * LANES
    bw = _bias_by_dist(rel_bias, tok8[:, None] + n_win - jnp.arange(nwp)[None, :])
    biasw_s = bw.reshape(NSA_GROUPS, NSA_HPG * 8, nwp)
    pool_s = (jnp.arange(nch_s)[:, None] // (SLC_BLOCK // CMP_STRIDE) == jnp.arange(nsl_pad)[None, :]).astype(BF16)
    n_pair = -(-n_slc_s // 2)
    ts = _bias_by_dist(rel_bias, past + jnp.arange(n_tok)[:, None, None]
                       - (jnp.arange(n_pair) * 2 * SLC_BLOCK)[None, :, None]
                       - jnp.arange(2 * SLC_BLOCK)[None, None, :])
    ts = ts.reshape(NSA_GROUPS, NSA_HPG, n_tok, n_pair, 2 * SLC_BLOCK).transpose(2, 0, 3, 1, 4)
    tbl_s = jnp.concatenate([ts, jnp.zeros_like(ts)], axis=3)

    mod = _ada(jnp.concatenate([c_prompt, c_sample], axis=0), w_ada, b_ada)
    ab_s = _compress_decode(page_flat, cache_cmp, pe16, w1lo, w1hi, db)

    xp = x_prompt.reshape(mp, d)
    xs = x_sample.reshape(ms, d)
    rows_p, rows_s = [], []
    for l in range(depth):
        shift, scale, gate = mod[l, :, 0:d], mod[l, :, d:2 * d], mod[l, :, 2 * d:3 * d]
        g_l, kvg_l, qg_l = norm_g[l][None], mla_kv_norm[l][None], mla_q_norm[l][None]

        (cq, ckv, ckvb, kpe, kpeb, gmla, qn, kvn, kvnb, br, gnsa, mrg) = _inproj(
            xp, g_l, scale[:nb, None], shift[:nb, None], kvg_l, cos_p, sin_p, w_in_p[l], per_row=False)
        qlat, qpe = _qprep(cq, qg_l, tile8(cos_p), tile8(sin_p), w_uq_p[l], w_ukt[l])
        o_a = _mla_prompt(qlat, qpe, ckvb, kpeb, w_uv[l], nb, t)
        ab_p = _compress_prompt(kvn, pe16[l], w1lo[l], w1hi[l], nb, t)
        o_b = _nsa_prompt(qn, kvnb, ab_p, br, biasc_p, tz, w2bd[l], pool_p, emat_p, nb, t)
        xp = _merge(xp, gate[:nb, None], o_a, gmla, o_b, gnsa, mrg, w_oa[l], w_ob[l], w_o[l], per_row=False)
        kv5 = kvn.reshape(nb, t, 3, KV_ROW)
        rows_p.append((ckv.reshape(nb, t, -1), kpe.reshape(nb, t, -1),
                       kv5[:, :, 0].reshape((nb, t) + kv_tail), kv5[:, :, 1].reshape((nb, t) + kv_tail),
                       kv5[:, t - min(WINDOW, t):, 2].reshape((nb, min(WINDOW, t)) + kv_tail)))

        rep = lambda a: jnp.repeat(a[nb:], n_tok, axis=0)
        (cq, ckv, _, kpe, _, gmla, qn, kvn, _, br, gnsa, mrg) = _inproj(
            xs, g_l, rep(scale), rep(shift), kvg_l, cos_s, sin_s, w_in_p[l], per_row=True)
        qlat, qpe = _qprep(cq, qg_l, tile8(cos_s), tile8(sin_s), w_uq_p[l], w_ukt[l])
        to_bht = lambda a: a.reshape(MLA_HEADS, db, n_tok, -1).transpose(1, 0, 2, 3).reshape(db, MLA_HEADS * n_tok, -1)
        pad16 = lambda a: jnp.pad(a.reshape(db, n_tok, -1), ((0, 0), (0, 16 - n_tok), (0, 0)))
        olat = _mla_decode(page_flat, to_bht(qlat), to_bht(qpe), cache_mla_ckv, cache_mla_kpe,
                           pad16(ckv), pad16(kpe), l, n_tok)
        olat = olat.reshape(db, MLA_HEADS, n_tok, -1).transpose(1, 0, 2, 3).reshape(MLA_HEADS, ms, -1)
        o_a = _uv(olat, w_uv[l])

        kv5 = kvn.reshape(db, n_tok, 3, KV_ROW)
        win_new = kv5[:, :, 2]
        win_cat = jnp.concatenate([state_nsa_win[l].reshape(db, n_win, KV_ROW), win_new], axis=1)
        win_all = jnp.pad(win_cat, ((0, 0), (0, nwp - n_win - n_tok), (0, 0)))
        q5 = qn.reshape(db, n_tok, NSA_GROUPS, NSA_HPG, NSA_HD)
        qa = jnp.pad(q5.transpose(0, 2, 3, 1, 4), ((0, 0), (0, 0), (0, 0), (0, 8 - n_tok), (0, 0)))
        qa = qa.reshape(db, NSA_GROUPS, NSA_HPG * 8, NSA_HD)
        oc, ow, idx = _nsa_decode_a(qa, ab_s, l, win_all, biasc_s, biasw_s, w2bd[l], pool_s,
                                    past, n_slc_s, n_win)
        idx_flat = idx[:, :, :n_tok, :N_SELECT].reshape(-1)
        qb = jnp.zeros((db, n_tok, NSA_GROUPS, 16, KV_ROW), BF16)
        for g in range(NSA_GROUPS):
            qb = qb.at[:, :, g, :NSA_HPG, NSA_HD * g:NSA_HD * (g + 1)].set(q5[:, :, g])
        new_slc = jnp.pad(kv5[:, :, 1], ((0, 0), (0, SLC_BLOCK - n_tok), (0, 0)))
        osb = _nsa_decode_b(idx_flat, page_flat, qb, cache_slc, new_slc, tbl_s, l, past)
        o_s = jnp.stack([osb[:, :, g, :NSA_HPG, 2 * NSA_HD + NSA_HD * g:2 * NSA_HD + NSA_HD * (g + 1)]
                         for g in range(NSA_GROUPS)], axis=2).reshape(ms, NSA_WIDTH)
        from_a = lambda a: (a.reshape(db, NSA_GROUPS, NSA_HPG, 8, NSA_HD)[:, :, :, :n_tok]
                            .transpose(0, 3, 1, 2, 4).reshape(ms, NSA_WIDTH))
        g3 = jnp.repeat(br.reshape(ms, NSA_HEADS, 3).transpose(0, 2, 1), NSA_HD, axis=-1).reshape(ms, 3 * NSA_WIDTH)
        o_b = _combine(from_a(oc), o_s, from_a(ow), g3)
        xs = _merge(xs, rep(gate), o_a, gmla, o_b, gnsa, mrg, w_oa[l], w_ob[l], w_o[l], per_row=True)
        rows_s.append((ckv.reshape(db, n_tok, -1), kpe.reshape(db, n_tok, -1),
                       kv5[:, :, 0].reshape((db, n_tok) + kv_tail), kv5[:, :, 1].reshape((db, n_tok) + kv_tail),
                       win_cat[:, n_tok:].reshape((db, n_win) + kv_tail)))

    p_out = [jnp.stack(r) for r in zip(*rows_p)]
    s_out = [jnp.stack(r) for r in zip(*rows_s)]
    fn = final_norm[None]
    y_prompt = _final_norm(xp, fn).reshape(nb, t, d)
    y_sample = _final_norm(xs, fn).reshape(db, n_tok, d)
    return (y_prompt, y_sample, *p_out, *s_out)
```
